```python
import numpy as np
import jax
import jax.numpy as jnp
from jax import lax

D_MODEL = 4096
BATCH = 2
SEQ = 8192
DEPTH = 4

HEAD_DIM = 128
Q_BLOCK = 128
RMS_EPS = 1e-5
N_MOD = 6
MOD_INIT_SCALE = 0.5

CONV_WIDTH = D_MODEL // 4
CONV_K = 3
NSA_WIDTH = D_MODEL // 2
NSA_HEADS = NSA_WIDTH // HEAD_DIM
NSA_KV_HEADS = 2
NSA_GROUP = NSA_HEADS // NSA_KV_HEADS
NSA_CMP_LEN = 32
NSA_CMP_STRIDE = 16
NSA_SLC_LEN = 64
NSA_N_SELECT = 16
NSA_WINDOW = 512
NSA_N_BRANCH = 3
DIL_WIDTH = D_MODEL - CONV_WIDTH - NSA_WIDTH
DIL_WINDOWS = (128, 512, 2048)
DIL_RATES = (1, 4, 16)
DIL_GROUPS = 3
DIL_HEADS = DIL_WIDTH // HEAD_DIM
DIL_KEYS = DIL_WINDOWS[0] // DIL_RATES[0] + 1
N_ALIBI_HEADS = DIL_GROUPS * DIL_HEADS + NSA_HEADS
N_EXPERTS = 32
TOP_K = 4
D_EXPERT = 256
SWIGLU_LIMIT = 7.0
SWIGLU_ALPHA = 1.702

IN_SIZES = (CONV_WIDTH,) * 3 + (NSA_HEADS * HEAD_DIM,) + (NSA_KV_HEADS * HEAD_DIM,) * 6 + (NSA_HEADS * NSA_N_BRANCH,) + (DIL_GROUPS * DIL_HEADS * HEAD_DIM,) + (DIL_GROUPS * HEAD_DIM,) * 2
IN_WIDTH = sum(IN_SIZES)
IN_SPLITS = tuple(int(s) for s in np.cumsum(IN_SIZES)[:-1])

kernel_name = 'hybrid_conv_nsa_dilated_moe_block'


def rms_norm(x, g):
    xf = x.astype(jnp.float32)
    y = xf * lax.rsqrt(jnp.mean(xf * xf, axis=-1, keepdims=True) + RMS_EPS)
    return (y * g.astype(jnp.float32)).astype(x.dtype)


def alibi_slopes():
    i = np.arange(1, N_ALIBI_HEADS + 1, dtype=np.float32)
    return np.exp2(-8.0 * i / N_ALIBI_HEADS).astype(np.float32)


def masked_softmax(s, mask):
    s = jnp.where(mask, s, -jnp.inf)
    m = jnp.max(s, axis=-1, keepdims=True)
    m = jnp.where(jnp.isfinite(m), m, 0.0)
    p = jnp.exp(s - m)
    den = jnp.sum(p, axis=-1, keepdims=True)
    lse = m + jnp.log(den)
    return p / jnp.maximum(den, jnp.finfo(jnp.float32).tiny), lse


def short_conv_mixer(b_gate, c_gate, h, w_conv):
    u = c_gate * h
    y = lax.conv_general_dilated(u, w_conv[:, None, :].astype(u.dtype), window_strides=(1,),
                                 padding=((CONV_K - 1, 0),), dimension_numbers=('NWC', 'WIO', 'NWC'),
                                 feature_group_count=u.shape[-1])
    return b_gate * y


def compress_blocks(kv, cmp_idx, w1, w2, pe):
    B, n_cmp, L = kv.shape[0], cmp_idx.shape[0], cmp_idx.shape[1]
    blocks = kv[:, cmp_idx] + pe[None, None, :, None, :]
    blocks = blocks.transpose(0, 1, 3, 2, 4).reshape(B, n_cmp, NSA_KV_HEADS, L * HEAD_DIM)
    return jax.nn.gelu(blocks @ w1) @ w2


def nsa_mixer(q, k_cmp, v_cmp, k_slc, v_slc, k_win, v_win, gate_logits,
              phi_k1, phi_k2, pe_k, phi_v1, phi_v2, pe_v, slopes):
    B, S = q.shape[0], q.shape[1]
    G, HG, Dh = NSA_KV_HEADS, NSA_GROUP, HEAD_DIM
    L, ST, LS, W = NSA_CMP_LEN, NSA_CMP_STRIDE, NSA_SLC_LEN, NSA_WINDOW
    scale = Dh ** -0.5
    q = q.reshape(B, S, G, HG, Dh)
    gates = jax.nn.sigmoid(gate_logits.astype(jnp.float32)).reshape(B, S, G, HG, NSA_N_BRANCH)
    k_cmp, v_cmp, k_slc, v_slc, k_win, v_win = [a.reshape(B, S, G, Dh) for a in (k_cmp, v_cmp, k_slc, v_slc, k_win, v_win)]
    n_cmp = (S - L) // ST + 1
    cmp_idx = np.arange(n_cmp)[:, None] * ST + np.arange(L)[None, :]
    cmp_end = cmp_idx[:, -1]
    cmp_center = (cmp_idx[:, 0] + (L - 1) / 2.0).astype(np.float32)
    kc = compress_blocks(k_cmp, cmp_idx, phi_k1, phi_k2, pe_k)
    vc = compress_blocks(v_cmp, cmp_idx, phi_v1, phi_v2, pe_v)
    n_slc = S // LS
    slc_start = np.arange(n_slc) * LS
    overlap = ((cmp_idx[:, :1] <= slc_start[None, :] + LS - 1) & (cmp_end[:, None] >= slc_start[None, :])).astype(np.float32)
    k_sel = min(NSA_N_SELECT, n_slc)
    kb = k_slc.reshape(B, n_slc, LS, G, Dh).transpose(0, 3, 1, 2, 4).reshape(B, G, n_slc, LS * Dh)
    vb = v_slc.reshape(B, n_slc, LS, G, Dh).transpose(0, 3, 1, 2, 4).reshape(B, G, n_slc, LS * Dh)
    pad = ((0, 0), (W, 0), (0, 0), (0, 0))
    kw_pad = jnp.pad(k_win, pad)
    vw_pad = jnp.pad(v_win, pad)
    sl = jnp.asarray(slopes)[None, :, :, None, None]
    b_ix = np.arange(B)[:, None, None, None]
    g_ix = np.arange(G)[None, :, None, None]
    blk_ids = np.arange(n_slc)

    def block(i):
        q0 = i * Q_BLOCK
        t = q0 + jnp.arange(Q_BLOCK)
        qb = lax.dynamic_slice_in_dim(q, q0, Q_BLOCK, axis=1).astype(jnp.float32)
        s_c = jnp.einsum('bqghd,bigd->bghqi', qb, kc) * scale - sl * (t[:, None] - cmp_center[None, :])
        p_c, _ = masked_softmax(s_c, cmp_end[None, :] <= t[:, None])
        o_c = jnp.einsum('bghqi,bigd->bqghd', p_c, vc)
        imp = jnp.einsum('bghqi,ij->bgqj', p_c, overlap)
        cur = (t // LS)[:, None]
        j = blk_ids[None, :]
        forced = (j == 0) | (j == cur) | (j == cur - 1)
        imp = jnp.where(forced, jnp.inf, jnp.where(j <= cur, imp, -jnp.inf))
        _, sel = lax.top_k(imp, k_sel)
        ks = kb[b_ix, g_ix, sel].reshape(B, G, Q_BLOCK, k_sel, LS, Dh)
        vs = vb[b_ix, g_ix, sel].reshape(B, G, Q_BLOCK, k_sel, LS, Dh)
        pos = sel[..., None] * LS + jnp.arange(LS)
        dist_s = (t[None, None, :, None, None] - pos).astype(jnp.float32)
        s_s = jnp.einsum('bqghd,bgqknd->bghqkn', qb, ks) * scale - sl[..., None] * dist_s[:, :, None]
        mask_s = (dist_s >= 0)[:, :, None].reshape(B, G, 1, Q_BLOCK, k_sel * LS)
        p_s, _ = masked_softmax(s_s.reshape(B, G, HG, Q_BLOCK, k_sel * LS), mask_s)
        o_s = jnp.einsum('bghqkn,bgqknd->bqghd', p_s.reshape(s_s.shape), vs)
        kw = lax.dynamic_slice_in_dim(kw_pad, q0, Q_BLOCK + W, axis=1)
        vw = lax.dynamic_slice_in_dim(vw_pad, q0, Q_BLOCK + W, axis=1)
        spos = q0 - W + jnp.arange(Q_BLOCK + W)
        dist_w = t[:, None] - spos[None, :]
        mask_w = (dist_w >= 0) & (dist_w < W) & (spos[None, :] >= 0)
        s_w = jnp.einsum('bqghd,bsgd->bghqs', qb, kw) * scale - sl * dist_w.astype(jnp.float32)
        p_w, _ = masked_softmax(s_w, mask_w)
        o_w = jnp.einsum('bghqs,bsgd->bqghd', p_w, vw)
        gb = lax.dynamic_slice_in_dim(gates, q0, Q_BLOCK, axis=1)
        o = gb[..., 0:1] * o_c + gb[..., 1:2] * o_s + gb[..., 2:3] * o_w
        return o.astype(q.dtype)

    out = lax.map(block, jnp.arange(S // Q_BLOCK))
    return out.transpose(1, 0, 2, 3, 4, 5).reshape(B, S, NSA_HEADS * Dh)


def dilated_mixer(q, k, v, slopes):
    B, S = q.shape[0], q.shape[1]
    Gd, Hd, Dh, J = DIL_GROUPS, DIL_HEADS, HEAD_DIM, DIL_KEYS
    scale = Dh ** -0.5
    q = q.reshape(B, S, Gd, Hd, Dh)
    kt = k.reshape(B, S, Gd, Dh).transpose(0, 2, 1, 3)
    vt = v.reshape(B, S, Gd, Dh).transpose(0, 2, 1, 3)
    offs = np.asarray(DIL_RATES)[:, None] * np.arange(J)[None, :]
    alibi = jnp.asarray(slopes)[None, :, :, None, None] * offs.astype(np.float32)[None, :, None, None, :]
    g_ix = np.arange(Gd)[:, None]

    def block(i):
        q0 = i * Q_BLOCK
        t = q0 + jnp.arange(Q_BLOCK)
        pos = t[None, :, None] - offs[:, None, :]
        idx = jnp.maximum(pos, 0).reshape(Gd, Q_BLOCK * J)
        kg = kt[:, g_ix, idx].reshape(B, Gd, Q_BLOCK, J, Dh)
        vg = vt[:, g_ix, idx].reshape(B, Gd, Q_BLOCK, J, Dh)
        qb = lax.dynamic_slice_in_dim(q, q0, Q_BLOCK, axis=1).astype(jnp.float32)
        s = jnp.einsum('bqghd,bgqjd->bghqj', qb, kg) * scale - alibi
        p, lse = masked_softmax(s, (pos >= 0)[None, :, None])
        o = jnp.einsum('bghqj,bgqjd->bghqd', p, vg)
        alpha = jax.nn.softmax(lse, axis=1)
        o = jnp.sum(alpha * o, axis=1)
        return o.transpose(0, 2, 1, 3).astype(q.dtype)

    out = lax.map(block, jnp.arange(S // Q_BLOCK))
    return out.transpose(1, 0, 2, 3, 4).reshape(B, S, Hd * Dh)


def moe_ffn(h, w_router, b_router, w_gate, b_gate, w_up, b_up, w_down, b_down):
    B, S, D = h.shape
    xt = h.reshape(B * S, D)
    logits = (xt @ w_router + b_router).astype(jnp.float32)
    top_v, top_i = lax.top_k(logits, TOP_K)
    top_w = jax.nn.softmax(top_v, axis=-1)
    combine = jnp.sum((top_i[..., None] == jnp.arange(N_EXPERTS)).astype(jnp.float32) * top_w[..., None], axis=1)
    combine = combine.astype(xt.dtype)
    y = jnp.zeros_like(xt)
    for e in range(N_EXPERTS):
        g = jnp.minimum(xt @ w_gate[e] + b_gate[e], SWIGLU_LIMIT)
        u = jnp.clip(xt @ w_up[e] + b_up[e], -SWIGLU_LIMIT, SWIGLU_LIMIT)
        act = g * jax.nn.sigmoid(SWIGLU_ALPHA * g) * (u + 1.0)
        y = y + combine[:, e:e + 1] * (act @ w_down[e] + b_down[e])
    return y.reshape(B, S, D)


def setup_inputs(seed: int = 0) -> dict:
    key = jax.random.key(seed)
    ks = jax.random.split(key, 26)
    L, D, E, F = DEPTH, D_MODEL, N_EXPERTS, D_EXPERT
    f32 = jnp.float32

    def nrm(k, shape, fan_in):
        return jax.random.normal(k, shape, f32) * (fan_in ** -0.5)

    def small(k, shape, s):
        return s * jax.random.normal(k, shape, f32)

    return {
        'x': jax.random.normal(ks[0], (BATCH, SEQ, D), f32),
        'c': jax.random.normal(ks[1], (BATCH, D), f32),
        'w_mod': nrm(ks[2], (D, N_MOD * D), D) * MOD_INIT_SCALE,
        'b_mod': small(ks[3], (N_MOD * D,), 0.02),
        'mod_table': small(ks[4], (L, N_MOD, D), 0.1),
        'norm_mix': 1.0 + small(ks[5], (L, D), 0.05),
        'w_in': nrm(ks[6], (L, D, IN_WIDTH), D),
        'conv_w': nrm(ks[7], (L, CONV_K, CONV_WIDTH), CONV_K),
        'phi_k1': nrm(ks[8], (L, NSA_CMP_LEN * HEAD_DIM, HEAD_DIM), NSA_CMP_LEN * HEAD_DIM),
        'phi_k2': nrm(ks[9], (L, HEAD_DIM, HEAD_DIM), HEAD_DIM),
        'pe_k': small(ks[10], (L, NSA_CMP_LEN, HEAD_DIM), 0.5),
        'phi_v1': nrm(ks[11], (L, NSA_CMP_LEN * HEAD_DIM, HEAD_DIM), NSA_CMP_LEN * HEAD_DIM),
        'phi_v2': nrm(ks[12], (L, HEAD_DIM, HEAD_DIM), HEAD_DIM),
        'pe_v': small(ks[13], (L, NSA_CMP_LEN, HEAD_DIM), 0.5),
        'w_out': nrm(ks[14], (L, D, D), D),
        'norm_ffn': 1.0 + small(ks[15], (L, D), 0.05),
        'w_router': nrm(ks[16], (L, D, E), D),
        'b_router': small(ks[17], (L, E), 0.01),
        'w_gate': nrm(ks[18], (L, E, D, F), D),
        'b_gate': small(ks[19], (L, E, F), 0.01),
        'w_up': nrm(ks[20], (L, E, D, F), D),
        'b_up': small(ks[21], (L, E, F), 0.01),
        'w_down': nrm(ks[22], (L, E, F, D), F),
        'b_down': small(ks[23], (L, E, D), 0.01),
        'norm_final': 1.0 + small(ks[24], (D,), 0.05),
    }


def reference(x, c, w_mod, b_mod, mod_table, norm_mix, w_in, conv_w, phi_k1, phi_k2, pe_k,
              phi_v1, phi_v2, pe_v, w_out, norm_ffn, w_router, b_router, w_gate, b_gate,
              w_up, b_up, w_down, b_down, norm_final):
    B, S, D = x.shape
    slopes = alibi_slopes()
    n_dil = DIL_GROUPS * DIL_HEADS
    slopes_dil = slopes[:n_dil].reshape(DIL_GROUPS, DIL_HEADS)
    slopes_nsa = slopes[n_dil:].reshape(NSA_KV_HEADS, NSA_GROUP)
    mod = (jax.nn.silu(c) @ w_mod + b_mod).reshape(B, N_MOD, D)
    for l in range(DEPTH):
        m = mod + mod_table[l]
        sh_a, sc_a, g_a, sh_f, sc_f, g_f = [m[:, i, None, :] for i in range(N_MOD)]
        h = rms_norm(x, norm_mix[l]) * (1.0 + sc_a) + sh_a
        z = h @ w_in[l]
        (a_b, a_c, a_h, n_q, n_kc, n_vc, n_ks, n_vs, n_kw, n_vw, n_gate, d_q, d_k, d_v) = jnp.split(z, IN_SPLITS, axis=-1)
        y_a = short_conv_mixer(a_b, a_c, a_h, conv_w[l])
        y_b = nsa_mixer(n_q, n_kc, n_vc, n_ks, n_vs, n_kw, n_vw, n_gate,
                        phi_k1[l], phi_k2[l], pe_k[l], phi_v1[l], phi_v2[l], pe_v[l], slopes_nsa)
        y_c = dilated_mixer(d_q, d_k, d_v, slopes_dil)
        y = jnp.concatenate([y_a, y_b, y_c], axis=-1) @ w_out[l]
        x = x + g_a * y
        h = rms_norm(x, norm_ffn[l]) * (1.0 + sc_f) + sh_f
        x = x + g_f * moe_ffn(h, w_router[l], b_router[l], w_gate[l], b_gate[l], w_up[l], b_up[l], w_down[l], b_down[l])
    return rms_norm(x, norm_final)
```

```python
import functools

import numpy as np
import jax
import jax.numpy as jnp
from jax import lax
from jax.experimental import pallas as pl
from jax.experimental.pallas import tpu as pltpu

F32 = jnp.float32
BF16 = jnp.bfloat16

HEAD_DIM = 128
Q_BLOCK = 128
RMS_EPS = 1e-5
N_MOD = 6
NSA_KV_HEADS = 2
NSA_CMP_LEN = 32
NSA_CMP_STRIDE = 16
NSA_SLC_LEN = 64
NSA_N_SELECT = 16
NSA_WINDOW = 512
NSA_N_BRANCH = 3
DIL_RATES = (1, 4, 16)
DIL_SPAN = 128
N_EXPERTS = 32
TOP_K = 4
SWIGLU_LIMIT = 7.0
SWIGLU_ALPHA = 1.702

LANES = 128
SUBLANES = 8
NEG = -1e30
TINY = float(np.finfo(np.float32).tiny)
MIB = 1024 * 1024

NT_DIMS = (((1,), (1,)), ((), ()))


def _cparams(n_axes, vmem_mib):
    return pltpu.CompilerParams(dimension_semantics=("arbitrary",) * n_axes,
                                vmem_limit_bytes=int(vmem_mib * MIB))


def _sigmoid(v):
    return 1.0 / (1.0 + jnp.exp(-v))


def _mod_kernel(ct_ref, w_ref, b_ref, o_ref):
    nb = ct_ref.shape[0]
    tn = w_ref.shape[1]
    for b in range(nb):
        cb = ct_ref[b]
        cb = cb * _sigmoid(cb)
        cols = [jnp.sum(w_ref[:, j * LANES:(j + 1) * LANES] * cb, axis=0, keepdims=True)
                for j in range(tn // LANES)]
        o_ref[b:b + 1, :] = jnp.concatenate(cols, axis=1) + b_ref[...]


def mod_projection(c, w_mod, b_mod, tn=512):
    nb, k = c.shape
    n = w_mod.shape[1]
    ct = jnp.broadcast_to(c[:, :, None], (nb, k, LANES))
    return pl.pallas_call(
        _mod_kernel,
        grid=(n // tn,),
        in_specs=[pl.BlockSpec((nb, k, LANES), lambda j: (0, 0, 0)),
                  pl.BlockSpec((k, tn), lambda j: (0, j)),
                  pl.BlockSpec((1, tn), lambda j: (0, j))],
        out_specs=pl.BlockSpec((nb, tn), lambda j: (0, j)),
        out_shape=jax.ShapeDtypeStruct((nb, n), F32),
        compiler_params=_cparams(1, 40),
        name="mod_projection",
    )(ct, w_mod, b_mod.reshape(1, n))


def _modulated_norm(x_ref, g_ref, mod_ref, tab_ref, shift_row, scale_row):
    x = x_ref[...]
    y = x * lax.rsqrt(jnp.mean(x * x, axis=-1, keepdims=True) + RMS_EPS) * g_ref[...]
    if mod_ref is None:
        return y
    m = mod_ref[...] + tab_ref[...]
    return y * (1.0 + m[scale_row:scale_row + 1, :]) + m[shift_row:shift_row + 1, :]


def _norm_kernel(x_ref, g_ref, mod_ref, tab_ref, o_ref, *, shift_row, scale_row):
    o_ref[...] = _modulated_norm(x_ref, g_ref, mod_ref, tab_ref, shift_row, scale_row).astype(o_ref.dtype)


def _final_norm_kernel(x_ref, g_ref, o_ref):
    o_ref[...] = _modulated_norm(x_ref, g_ref, None, None, 0, 0)


def _norm_router_kernel(x_ref, g_ref, mod_ref, tab_ref, wr_ref, br_ref, h_ref, idx_ref, wt_ref,
                        *, shift_row, scale_row):
    h = _modulated_norm(x_ref, g_ref, mod_ref, tab_ref, shift_row, scale_row)
    h_ref[...] = h
    logits = jnp.dot(h, wr_ref[...], preferred_element_type=F32, precision=lax.Precision.HIGHEST) + br_ref[...]
    lane = lax.broadcasted_iota(jnp.int32, logits.shape, 1)
    v = jnp.where(lane < N_EXPERTS, logits, -jnp.inf)
    idx_out = jnp.zeros(logits.shape, jnp.int32)
    val_out = jnp.full(logits.shape, -jnp.inf, F32)
    for k in range(TOP_K):
        mx = jnp.max(v, axis=-1, keepdims=True)
        first = jnp.min(jnp.where(v == mx, lane, LANES), axis=-1, keepdims=True)
        idx_out = jnp.where(lane == k, first, idx_out)
        val_out = jnp.where(lane == k, mx, val_out)
        v = jnp.where(lane == first, -jnp.inf, v)
    top = jnp.max(val_out, axis=-1, keepdims=True)
    e = jnp.where(lane < TOP_K, jnp.exp(val_out - top), 0.0)
    idx_ref[...] = idx_out
    wt_ref[...] = e / jnp.sum(e, axis=-1, keepdims=True)


def _norm_specs(tm, d, seq):
    row = lambda i: (i, 0)
    return [pl.BlockSpec((tm, d), row),
            pl.BlockSpec((1, d), lambda i: (0, 0)),
            pl.BlockSpec((None, N_MOD, d), lambda i: ((i * tm) // seq, 0, 0)),
            pl.BlockSpec((N_MOD, d), lambda i: (0, 0))]


def modulated_norm(x2, g, mod, tab, seq, shift_row, scale_row, out_dtype, tm=256):
    n, d = x2.shape
    return pl.pallas_call(
        functools.partial(_norm_kernel, shift_row=shift_row, scale_row=scale_row),
        grid=(n // tm,),
        in_specs=_norm_specs(tm, d, seq),
        out_specs=pl.BlockSpec((tm, d), lambda i: (i, 0)),
        out_shape=jax.ShapeDtypeStruct((n, d), out_dtype),
        compiler_params=_cparams(1, 40),
        name="modulated_norm",
    )(x2, g.reshape(1, d), mod, tab)


def final_norm(x2, g, tm=256):
    n, d = x2.shape
    return pl.pallas_call(
        _final_norm_kernel,
        grid=(n // tm,),
        in_specs=[pl.BlockSpec((tm, d), lambda i: (i, 0)), pl.BlockSpec((1, d), lambda i: (0, 0))],
        out_specs=pl.BlockSpec((tm, d), lambda i: (i, 0)),
        out_shape=jax.ShapeDtypeStruct((n, d), F32),
        compiler_params=_cparams(1, 40),
        name="final_norm",
    )(x2, g.reshape(1, d))


def norm_router(x2, g, mod, tab, w_router, b_router, seq, shift_row, scale_row, tm=256):
    n, d = x2.shape
    e = w_router.shape[1]
    wr = jnp.pad(w_router, ((0, 0), (0, LANES - e)))
    br = jnp.pad(b_router, (0, LANES - e)).reshape(1, LANES)
    row = lambda i: (i, 0)
    return pl.pallas_call(
        functools.partial(_norm_router_kernel, shift_row=shift_row, scale_row=scale_row),
        grid=(n // tm,),
        in_specs=_norm_specs(tm, d, seq) + [pl.BlockSpec((d, LANES), lambda i: (0, 0)),
                                            pl.BlockSpec((1, LANES), lambda i: (0, 0))],
        out_specs=[pl.BlockSpec((tm, d), row), pl.BlockSpec((tm, LANES), row), pl.BlockSpec((tm, LANES), row)],
        out_shape=[jax.ShapeDtypeStruct((n, d), F32),
                   jax.ShapeDtypeStruct((n, LANES), jnp.int32),
                   jax.ShapeDtypeStruct((n, LANES), F32)],
        compiler_params=_cparams(1, 48),
        name="norm_router",
    )(x2, g.reshape(1, d), mod, tab, wr, br)


def _mm_kernel(a_ref, w_ref, o_ref):
    o_ref[...] = jnp.dot(a_ref[...], w_ref[...], preferred_element_type=F32).astype(o_ref.dtype)


def matmul(a, w, out_dtype, tm=512, tn=512):
    m, k = a.shape
    n = w.shape[1]
    tn = min(tn, n)
    return pl.pallas_call(
        _mm_kernel,
        grid=(n // tn, m // tm),
        in_specs=[pl.BlockSpec((tm, k), lambda j, i: (i, 0)),
                  pl.BlockSpec((k, tn), lambda j, i: (0, j))],
        out_specs=pl.BlockSpec((tm, tn), lambda j, i: (i, j)),
        out_shape=jax.ShapeDtypeStruct((m, n), out_dtype),
        compiler_params=_cparams(2, 40),
        name="in_projection",
    )(a, w)


def _out_proj_kernel(ya_ref, yb_ref, yc_ref, w_ref, x_ref, mod_ref, tab_ref, o_ref, *, gate_row):
    ka, kb = ya_ref.shape[1], yb_ref.shape[1]
    acc = jnp.dot(ya_ref[...], w_ref[0:ka, :], preferred_element_type=F32)
    acc += jnp.dot(yb_ref[...], w_ref[ka:ka + kb, :], preferred_element_type=F32)
    acc += jnp.dot(yc_ref[...], w_ref[ka + kb:, :], preferred_element_type=F32)
    gate = mod_ref[gate_row:gate_row + 1, :] + tab_ref[gate_row:gate_row + 1, :]
    o_ref[...] = x_ref[...] + gate * acc


def out_projection(ya, yb, yc, w, x2, mod, tab, seq, gate_row, tm=512, tn=512):
    n, d = x2.shape
    return pl.pallas_call(
        functools.partial(_out_proj_kernel, gate_row=gate_row),
        grid=(d // tn, n // tm),
        in_specs=[pl.BlockSpec((tm, ya.shape[1]), lambda j, i: (i, 0)),
                  pl.BlockSpec((tm, yb.shape[1]), lambda j, i: (i, 0)),
                  pl.BlockSpec((tm, yc.shape[1]), lambda j, i: (i, 0)),
                  pl.BlockSpec((d, tn), lambda j, i: (0, j)),
                  pl.BlockSpec((tm, tn), lambda j, i: (i, j)),
                  pl.BlockSpec((None, N_MOD, tn), lambda j, i: ((i * tm) // seq, 0, j)),
                  pl.BlockSpec((N_MOD, tn), lambda j, i: (0, j))],
        out_specs=pl.BlockSpec((tm, tn), lambda j, i: (i, j)),
        out_shape=jax.ShapeDtypeStruct((n, d), F32),
        compiler_params=_cparams(2, 40),
        name="out_projection",
    )(ya, yb, yc, w, x2, mod, tab)


def _conv_kernel(b_ref, c_ref, h_ref, hc_ref, hh_ref, w_ref, o_ref, buf_ref, *, seq):
    ts = c_ref.shape[0]
    i = pl.program_id(0)
    keep = jnp.where((i * ts) % seq == 0, 0.0, 1.0)
    u = c_ref[...] * h_ref[...]
    buf_ref[0:SUBLANES, :] = hc_ref[...] * hh_ref[...] * keep
    buf_ref[SUBLANES:, :] = u
    y = (w_ref[0:1, :] * buf_ref[SUBLANES - 2:SUBLANES - 2 + ts, :]
         + w_ref[1:2, :] * buf_ref[SUBLANES - 1:SUBLANES - 1 + ts, :]
         + w_ref[2:3, :] * u)
    o_ref[...] = (b_ref[...] * y).astype(o_ref.dtype)


def short_conv(zc, conv_w, seq, ts=512):
    n = zc.shape[0]
    cw = conv_w.shape[1]
    halo = lambda col: pl.BlockSpec((SUBLANES, cw), lambda i: (jnp.maximum(i * (ts // SUBLANES) - 1, 0), col))
    return pl.pallas_call(
        functools.partial(_conv_kernel, seq=seq),
        grid=(n // ts,),
        in_specs=[pl.BlockSpec((ts, cw), lambda i: (i, 0)),
                  pl.BlockSpec((ts, cw), lambda i: (i, 1)),
                  pl.BlockSpec((ts, cw), lambda i: (i, 2)),
                  halo(1), halo(2),
                  pl.BlockSpec((3, cw), lambda i: (0, 0))],
        out_specs=pl.BlockSpec((ts, cw), lambda i: (i, 0)),
        out_shape=jax.ShapeDtypeStruct((n, cw), BF16),
        scratch_shapes=[pltpu.VMEM((ts + SUBLANES, cw), F32)],
        compiler_params=_cparams(1, 40),
        name="short_conv",
    )(zc, zc, zc, zc, zc, conv_w)


def _gelu_tanh(v):
    return 0.5 * v * (1.0 + jnp.tanh(np.sqrt(2.0 / np.pi).astype(np.float32) * (v + 0.044715 * (v * v * v))))


def _compress_kernel(x_ref, w1_ref, w2_ref, pe_ref, o_ref):
    x = x_ref[...]
    half = x.shape[1]
    nrow = x.shape[0]
    first = jnp.dot(x, w1_ref[0:half, :], preferred_element_type=F32)
    second = jnp.dot(x, w1_ref[half:, :], preferred_element_type=F32)
    pe_term = jnp.dot(pe_ref[...], w1_ref[...], preferred_element_type=F32)[0:1, :]
    hid = first + pltpu.roll(second, nrow - 1, 0) + pe_term
    o_ref[...] = jnp.dot(_gelu_tanh(hid).astype(BF16), w2_ref[...], preferred_element_type=F32).astype(o_ref.dtype)


def compress_tokens(chunks, w1, w2, pe):
    nb, g, nc, width = chunks.shape
    pe8 = jnp.broadcast_to(pe.reshape(1, -1), (SUBLANES, 2 * width)).astype(BF16)
    return pl.pallas_call(
        _compress_kernel,
        grid=(nb, g),
        in_specs=[pl.BlockSpec((None, None, nc, width), lambda b, h: (b, h, 0, 0)),
                  pl.BlockSpec((2 * width, HEAD_DIM), lambda b, h: (0, 0)),
                  pl.BlockSpec((HEAD_DIM, HEAD_DIM), lambda b, h: (0, 0)),
                  pl.BlockSpec((SUBLANES, 2 * width), lambda b, h: (0, 0))],
        out_specs=pl.BlockSpec((None, None, nc, HEAD_DIM), lambda b, h: (b, h, 0, 0)),
        out_shape=jax.ShapeDtypeStruct((nb, g, nc, HEAD_DIM), BF16),
        compiler_params=_cparams(2, 40),
        name="nsa_compress",
    )(chunks, w1.astype(BF16), w2.astype(BF16), pe8)


def _nsa_kernel(q_ref, kc_ref, vc_ref, ks_ref, vs_ref, kw_ref, vw_ref, gate_ref, slope_ref, o_ref,
                qs_ref, s_ref, p_ref, m_ref, l_ref, acc_ref, oc_ref, os_ref, imp_ref, *, hg, scale):
    qi = pl.program_id(2)
    q0 = qi * Q_BLOCK
    ncb = kc_ref.shape[0]
    nslc = ks_ref.shape[0] // NSA_SLC_LEN

    for h in range(hg):
        qs_ref[h * Q_BLOCK:(h + 1) * Q_BLOCK, :] = q_ref[:, h * HEAD_DIM:(h + 1) * HEAD_DIM]

    t_col = q0 + lax.broadcasted_iota(jnp.int32, (Q_BLOCK, 1), 0)

    ci = lax.broadcasted_iota(jnp.int32, (1, ncb), 1)
    mask_c = (ci * NSA_CMP_STRIDE + (NSA_CMP_LEN - 1)) <= t_col
    dist_c = t_col.astype(F32) - (ci.astype(F32) * NSA_CMP_STRIDE + (NSA_CMP_LEN - 1) / 2.0)
    imp_ref[...] = jnp.zeros(imp_ref.shape, F32)

    def cmp_head(h, carry):
        r0 = pl.multiple_of(h * Q_BLOCK, Q_BLOCK)
        qh = qs_ref[pl.ds(r0, Q_BLOCK), :]
        sl = slope_ref[pl.ds(r0, Q_BLOCK), 0:1]
        s = lax.dot_general(qh, kc_ref[...], NT_DIMS, preferred_element_type=F32) * scale - sl * dist_c
        s = jnp.where(mask_c, s, NEG)
        mx = jnp.max(s, axis=-1, keepdims=True)
        p = jnp.where(mask_c, jnp.exp(s - mx), 0.0)
        p = p / jnp.maximum(jnp.sum(p, axis=-1, keepdims=True), TINY)
        oc_ref[pl.ds(r0, Q_BLOCK), :] = jnp.dot(p.astype(BF16), vc_ref[...], preferred_element_type=F32)
        imp_ref[...] += p
        return carry

    lax.fori_loop(0, hg, cmp_head, 0)

    oi = lax.broadcasted_iota(jnp.int32, (ncb, nslc), 0)
    oj = lax.broadcasted_iota(jnp.int32, (ncb, nslc), 1)
    ratio = NSA_SLC_LEN // NSA_CMP_STRIDE
    overlap = jnp.where((oi <= ratio * oj + (ratio - 1)) & (oi >= ratio * oj - (NSA_CMP_LEN // NSA_CMP_STRIDE - 1)),
                        1.0, 0.0)
    imp = jnp.dot(imp_ref[...], overlap, preferred_element_type=F32, precision=lax.Precision.HIGHEST)
    blk = lax.broadcasted_iota(jnp.int32, (Q_BLOCK, nslc), 1)
    cur = t_col // NSA_SLC_LEN
    forced = (blk == 0) | (blk == cur) | (blk == cur - 1)
    visible = blk <= cur
    v = jnp.where(forced, jnp.inf, jnp.where(visible, imp, -jnp.inf))
    sel = jnp.zeros((Q_BLOCK, nslc), F32)
    for _ in range(min(NSA_N_SELECT, nslc)):
        mx = jnp.max(v, axis=-1, keepdims=True)
        first = jnp.min(jnp.where(v == mx, blk, nslc), axis=-1, keepdims=True)
        pick = blk == first
        sel = jnp.where(pick, 1.0, sel)
        v = jnp.where(pick, -jnp.inf, v)
    sel_b = jnp.where(visible, sel, 0.0).astype(BF16)

    def flash_reset():
        m_ref[...] = jnp.full(m_ref.shape, NEG, F32)
        l_ref[...] = jnp.zeros(l_ref.shape, F32)
        acc_ref[...] = jnp.zeros(acc_ref.shape, F32)

    def flash_tile(k_tile, v_tile, mask, kpos_f):
        s_ref[...] = lax.dot_general(qs_ref[...], k_tile, NT_DIMS, preferred_element_type=F32)
        for h in range(hg):
            rows = slice(h * Q_BLOCK, (h + 1) * Q_BLOCK)
            s = s_ref[rows, :] * scale + slope_ref[rows, :] * kpos_f
            s = jnp.where(mask, s, NEG)
            m_prev = m_ref[rows, :]
            m_new = jnp.maximum(m_prev, jnp.max(s, axis=-1, keepdims=True))
            alpha = jnp.exp(m_prev - m_new)
            p = jnp.where(mask, jnp.exp(s - m_new), 0.0)
            l_ref[rows, :] = alpha * l_ref[rows, :] + jnp.sum(p, axis=-1, keepdims=True)
            m_ref[rows, :] = m_new
            acc_ref[rows, :] = alpha * acc_ref[rows, :]
            p_ref[rows, :] = p.astype(BF16)
        acc_ref[...] += jnp.dot(p_ref[...], v_tile, preferred_element_type=F32)

    bj = lax.broadcasted_iota(jnp.int32, (nslc, Q_BLOCK), 0)

    def sel_tile(kt, carry):
        k0 = pl.multiple_of(kt * Q_BLOCK, Q_BLOCK)
        kpos = k0 + lax.broadcasted_iota(jnp.int32, (1, Q_BLOCK), 1)
        expand = jnp.where(bj == kpos // NSA_SLC_LEN, 1.0, 0.0).astype(BF16)
        chosen = jnp.dot(sel_b, expand, preferred_element_type=F32) > 0.5
        mask = chosen & (kpos <= t_col)
        flash_tile(ks_ref[pl.ds(k0, Q_BLOCK), :], vs_ref[pl.ds(k0, Q_BLOCK), :], mask, kpos.astype(F32))
        return carry

    flash_reset()
    lax.fori_loop(0, qi + 1, sel_tile, 0)
    os_ref[...] = acc_ref[...] / jnp.maximum(l_ref[...], TINY)

    def win_tile(kt, carry):
        k0 = pl.multiple_of(kt * Q_BLOCK, Q_BLOCK)
        kpos = k0 + lax.broadcasted_iota(jnp.int32, (1, Q_BLOCK), 1)
        dist = t_col - kpos
        mask = (dist >= 0) & (dist < NSA_WINDOW)
        flash_tile(kw_ref[pl.ds(k0, Q_BLOCK), :], vw_ref[pl.ds(k0, Q_BLOCK), :], mask, kpos.astype(F32))
        return carry

    flash_reset()
    lax.fori_loop(jnp.maximum(qi - NSA_WINDOW // Q_BLOCK, 0), qi + 1, win_tile, 0)

    gate = _sigmoid(gate_ref[...])
    for h in range(hg):
        rows = slice(h * Q_BLOCK, (h + 1) * Q_BLOCK)
        o_w = acc_ref[rows, :] / jnp.maximum(l_ref[rows, :], TINY)
        o = (gate[:, h:h + 1] * oc_ref[rows, :]
             + gate[:, hg + h:hg + h + 1] * os_ref[rows, :]
             + gate[:, 2 * hg + h:2 * hg + h + 1] * o_w)
        o_ref[:, h * HEAD_DIM:(h + 1) * HEAD_DIM] = o.astype(o_ref.dtype)


def nsa_attention(zn, kc, vc, zg, slopes, nb, seq, hg):
    g = NSA_KV_HEADS
    n = zn.shape[0]
    nq = seq // Q_BLOCK
    rows = hg * Q_BLOCK
    qcols = g * hg
    ncb = kc.shape[2]
    slope_rows = jnp.asarray(np.repeat(np.repeat(slopes.reshape(g, hg, 1), Q_BLOCK, axis=1), LANES, axis=2)
                             .reshape(g, rows, LANES))
    kv = lambda which: pl.BlockSpec((seq, HEAD_DIM), lambda b, h, i: (b, qcols + which * g + h))
    cmp_spec = pl.BlockSpec((None, None, ncb, HEAD_DIM), lambda b, h, i: (b, h, 0, 0))
    return pl.pallas_call(
        functools.partial(_nsa_kernel, hg=hg, scale=HEAD_DIM ** -0.5),
        grid=(nb, g, nq),
        in_specs=[pl.BlockSpec((Q_BLOCK, hg * HEAD_DIM), lambda b, h, i: (b * nq + i, h)),
                  cmp_spec, cmp_spec,
                  kv(2), kv(3), kv(4), kv(5),
                  pl.BlockSpec((Q_BLOCK, LANES), lambda b, h, i: (b * nq + i, h)),
                  pl.BlockSpec((None, rows, LANES), lambda b, h, i: (h, 0, 0))],
        out_specs=pl.BlockSpec((Q_BLOCK, hg * HEAD_DIM), lambda b, h, i: (b * nq + i, h)),
        out_shape=jax.ShapeDtypeStruct((n, g * hg * HEAD_DIM), BF16),
        scratch_shapes=[pltpu.VMEM((rows, HEAD_DIM), BF16),
                        pltpu.VMEM((rows, Q_BLOCK), F32),
                        pltpu.VMEM((rows, Q_BLOCK), BF16),
                        pltpu.VMEM((rows, LANES), F32),
                        pltpu.VMEM((rows, LANES), F32),
                        pltpu.VMEM((rows, HEAD_DIM), F32),
                        pltpu.VMEM((rows, HEAD_DIM), F32),
                        pltpu.VMEM((rows, HEAD_DIM), F32),
                        pltpu.VMEM((Q_BLOCK, ncb), F32)],
        compiler_params=_cparams(3, 48),
        name="nsa_attention",
    )(zn, kc, vc, zn, zn, zn, zn, zg, slope_rows)


def _dilated_kernel(q_ref, kp_ref, kc_ref, vp_ref, vc_ref, slope_ref, o_ref, lse_ref,
                    qs_ref, s_ref, p_ref, l_ref, *, hg, rate, scale):
    qb = pl.program_id(2)
    for h in range(hg):
        qs_ref[h * Q_BLOCK:(h + 1) * Q_BLOCK, :] = q_ref[:, h * HEAD_DIM:(h + 1) * HEAD_DIM]
    keys = jnp.concatenate([kp_ref[...], kc_ref[...]], axis=0)
    vals = jnp.concatenate([vp_ref[...], vc_ref[...]], axis=0)
    s_ref[...] = lax.dot_general(qs_ref[...], keys, NT_DIMS, preferred_element_type=F32)
    tq = qb * Q_BLOCK + lax.broadcasted_iota(jnp.int32, (Q_BLOCK, 1), 0)
    kpos = (qb - 1) * Q_BLOCK + lax.broadcasted_iota(jnp.int32, (1, 2 * Q_BLOCK), 1)
    dist = tq - kpos
    mask = (dist >= 0) & (dist <= DIL_SPAN) & (kpos >= 0)
    dist_f = dist.astype(F32) * float(rate)
    lane = lax.broadcasted_iota(jnp.int32, (Q_BLOCK, LANES), 1)
    lse_tile = jnp.zeros((Q_BLOCK, LANES), F32)
    for h in range(hg):
        rows = slice(h * Q_BLOCK, (h + 1) * Q_BLOCK)
        s = s_ref[rows, :] * scale - slope_ref[rows, 0:1] * dist_f
        s = jnp.where(mask, s, NEG)
        mx = jnp.max(s, axis=-1, keepdims=True)
        p = jnp.where(mask, jnp.exp(s - mx), 0.0)
        den = jnp.sum(p, axis=-1, keepdims=True)
        p_ref[rows, :] = p.astype(BF16)
        l_ref[rows, :] = jnp.broadcast_to(den, (Q_BLOCK, LANES))
        lse_tile = jnp.where(lane == h, mx + jnp.log(den), lse_tile)
    acc = jnp.dot(p_ref[...], vals, preferred_element_type=F32)
    for h in range(hg):
        rows = slice(h * Q_BLOCK, (h + 1) * Q_BLOCK)
        o_ref[:, h * HEAD_DIM:(h + 1) * HEAD_DIM] = acc[rows, :] / jnp.maximum(l_ref[rows, :], TINY)
    lse_ref[...] = lse_tile


def dilated_group(zdq, zdkv, slopes, group, nb, seq, hg):
    n_groups = len(DIL_RATES)
    rate = DIL_RATES[group]
    n = zdq.shape[0]
    ns = seq // rate
    nq = ns // Q_BLOCK
    rows = hg * Q_BLOCK
    qw, kvw = zdq.shape[1], zdkv.shape[1]
    qv = zdq.reshape(n // rate, rate * qw)
    kvv = zdkv.reshape(n // rate, rate * kvw)
    kv_cols = kvw // HEAD_DIM
    slope_rows = jnp.asarray(np.repeat(np.repeat(slopes[group].reshape(hg, 1), Q_BLOCK, axis=0), LANES, axis=1))
    prev = lambda b, r, i: jnp.maximum(b * nq + i - 1, 0)
    out, lse = pl.pallas_call(
        functools.partial(_dilated_kernel, hg=hg, rate=rate, scale=HEAD_DIM ** -0.5),
        grid=(nb, rate, nq),
        in_specs=[pl.BlockSpec((Q_BLOCK, hg * HEAD_DIM), lambda b, r, i: (b * nq + i, r * n_groups + group)),
                  pl.BlockSpec((Q_BLOCK, HEAD_DIM), lambda b, r, i: (prev(b, r, i), r * kv_cols + group)),
                  pl.BlockSpec((Q_BLOCK, HEAD_DIM), lambda b, r, i: (b * nq + i, r * kv_cols + group)),
                  pl.BlockSpec((Q_BLOCK, HEAD_DIM), lambda b, r, i: (prev(b, r, i), r * kv_cols + n_groups + group)),
                  pl.BlockSpec((Q_BLOCK, HEAD_DIM), lambda b, r, i: (b * nq + i, r * kv_cols + n_groups + group)),
                  pl.BlockSpec((rows, LANES), lambda b, r, i: (0, 0))],
        out_specs=[pl.BlockSpec((Q_BLOCK, hg * HEAD_DIM), lambda b, r, i: (b * nq + i, r)),
                   pl.BlockSpec((Q_BLOCK, LANES), lambda b, r, i: (b * nq + i, r))],
        out_shape=[jax.ShapeDtypeStruct((n // rate, rate * hg * HEAD_DIM), F32),
                   jax.ShapeDtypeStruct((n // rate, rate * LANES), F32)],
        scratch_shapes=[pltpu.VMEM((rows, HEAD_DIM), BF16),
                        pltpu.VMEM((rows, 2 * Q_BLOCK), F32),
                        pltpu.VMEM((rows, 2 * Q_BLOCK), BF16),
                        pltpu.VMEM((rows, LANES), F32)],
        compiler_params=_cparams(3, 40),
        name="dilated_attention",
    )(qv, kvv, kvv, kvv, kvv, slope_rows)
    return out.reshape(n, hg * HEAD_DIM), lse.reshape(n, LANES)


def _dilated_merge_kernel(o0_ref, o1_ref, o2_ref, l0_ref, l1_ref, l2_ref, y_ref, *, hg):
    l0, l1, l2 = l0_ref[...], l1_ref[...], l2_ref[...]
    mx = jnp.maximum(jnp.maximum(l0, l1), l2)
    e0, e1, e2 = jnp.exp(l0 - mx), jnp.exp(l1 - mx), jnp.exp(l2 - mx)
    inv = 1.0 / (e0 + e1 + e2)
    a0, a1, a2 = e0 * inv, e1 * inv, e2 * inv
    for h in range(hg):
        cols = slice(h * HEAD_DIM, (h + 1) * HEAD_DIM)
        y = (a0[:, h:h + 1] * o0_ref[:, cols] + a1[:, h:h + 1] * o1_ref[:, cols] + a2[:, h:h + 1] * o2_ref[:, cols])
        y_ref[:, cols] = y.astype(y_ref.dtype)


def dilated_merge(outs, lses, hg, tm=256):
    n, w = outs[0].shape
    row = lambda i: (i, 0)
    return pl.pallas_call(
        functools.partial(_dilated_merge_kernel, hg=hg),
        grid=(n // tm,),
        in_specs=[pl.BlockSpec((tm, w), row)] * 3 + [pl.BlockSpec((tm, LANES), row)] * 3,
        out_specs=pl.BlockSpec((tm, w), row),
        out_shape=jax.ShapeDtypeStruct((n, w), BF16),
        compiler_params=_cparams(1, 40),
        name="dilated_merge",
    )(*outs, *lses)


def _row_gather(src_hbm, idx_ref, base, count, dst_ref, sem):
    def issue(r, carry):
        pltpu.make_async_copy(src_hbm.at[pl.ds(idx_ref[base + r], 1), :], dst_ref.at[pl.ds(r, 1), :], sem).start()
        return carry
    lax.fori_loop(0, count, issue, 0)


def _row_gather_wait(src_hbm, count, dst_ref, sem):
    pltpu.make_async_copy(src_hbm.at[pl.ds(0, count), :], dst_ref, sem).wait()


def _expert_kernel(texp_ref, tok_ref, nused_ref, h_hbm, wg_ref, bg_ref, wu_ref, bu_ref, wd_ref, bd_ref, o_ref,
                   xbuf_ref, sem_ref):
    i = pl.program_id(0)
    nt = pl.num_programs(0)
    tm = o_ref.shape[0]
    slot = i % 2

    @pl.when(i == 0)
    def _():
        _row_gather(h_hbm, tok_ref, 0, tm, xbuf_ref.at[0], sem_ref.at[0])

    @pl.when(i + 1 < nt)
    def _():
        _row_gather(h_hbm, tok_ref, (i + 1) * tm, tm, xbuf_ref.at[1 - slot], sem_ref.at[1 - slot])

    _row_gather_wait(h_hbm, tm, xbuf_ref.at[slot], sem_ref.at[slot])

    @pl.when(i < nused_ref[0])
    def _():
        x = xbuf_ref[slot].astype(BF16)
        g = jnp.minimum(jnp.dot(x, wg_ref[...], preferred_element_type=F32) + bg_ref[...], SWIGLU_LIMIT)
        u = jnp.clip(jnp.dot(x, wu_ref[...], preferred_element_type=F32) + bu_ref[...], -SWIGLU_LIMIT, SWIGLU_LIMIT)
        act = g * _sigmoid(SWIGLU_ALPHA * g) * (u + 1.0)
        o_ref[...] = jnp.dot(act.astype(BF16), wd_ref[...], preferred_element_type=F32) + bd_ref[...]

    @pl.when(i >= nused_ref[0])
    def _():
        o_ref[...] = jnp.zeros(o_ref.shape, o_ref.dtype)


def expert_ffn(h2, tile_expert, row_token, n_used, wg, bg, wu, bu, wd, bd, tm):
    n, d = h2.shape
    e, _, f = wg.shape
    p = row_token.shape[0]
    by_expert = lambda shape: pl.BlockSpec((None,) + shape, lambda i, te, tok, nu: (te[i], 0, 0))
    grid_spec = pltpu.PrefetchScalarGridSpec(
        num_scalar_prefetch=3,
        grid=(p // tm,),
        in_specs=[pl.BlockSpec(memory_space=pl.ANY),
                  by_expert((d, f)), by_expert((1, f)), by_expert((d, f)), by_expert((1, f)),
                  by_expert((f, d)), by_expert((1, d))],
        out_specs=pl.BlockSpec((tm, d), lambda i, te, tok, nu: (i, 0)),
        scratch_shapes=[pltpu.VMEM((2, tm, d), F32), pltpu.SemaphoreType.DMA((2,))])
    return pl.pallas_call(
        _expert_kernel,
        grid_spec=grid_spec,
        out_shape=jax.ShapeDtypeStruct((p, d), F32),
        compiler_params=_cparams(1, 48),
        name="expert_ffn",
    )(tile_expert, row_token, n_used, h2, wg, bg.reshape(e, 1, f), wu, bu.reshape(e, 1, f), wd, bd.reshape(e, 1, d))


def _combine_kernel(pos_ref, y_hbm, wt_ref, x_ref, mod_ref, tab_ref, o_ref, buf_ref, sem_ref, *, gate_row):
    i = pl.program_id(0)
    nt = pl.num_programs(0)
    tt = o_ref.shape[0]
    slot = i % 2

    def gather(tile, s):
        for k in range(TOP_K):
            _row_gather(y_hbm, pos_ref, (k * nt + tile) * tt, tt, buf_ref.at[s, k], sem_ref.at[s])

    @pl.when(i == 0)
    def _():
        gather(0, 0)

    @pl.when(i + 1 < nt)
    def _():
        gather(i + 1, 1 - slot)

    for k in range(TOP_K):
        _row_gather_wait(y_hbm, tt, buf_ref.at[slot, k], sem_ref.at[slot])

    wt = wt_ref[...]
    y = wt[:, 0:1] * buf_ref[slot, 0]
    for k in range(1, TOP_K):
        y += wt[:, k:k + 1] * buf_ref[slot, k]
    gate = mod_ref[gate_row:gate_row + 1, :] + tab_ref[gate_row:gate_row + 1, :]
    o_ref[...] = x_ref[...] + gate * y


def expert_combine(y_sorted, pos, top_w, x2, mod, tab, seq, gate_row, tt=128):
    n, d = x2.shape
    row = lambda i, p: (i, 0)
    grid_spec = pltpu.PrefetchScalarGridSpec(
        num_scalar_prefetch=1,
        grid=(n // tt,),
        in_specs=[pl.BlockSpec(memory_space=pl.ANY),
                  pl.BlockSpec((tt, LANES), row),
                  pl.BlockSpec((tt, d), row),
                  pl.BlockSpec((None, N_MOD, d), lambda i, p: ((i * tt) // seq, 0, 0)),
                  pl.BlockSpec((N_MOD, d), lambda i, p: (0, 0))],
        out_specs=pl.BlockSpec((tt, d), row),
        scratch_shapes=[pltpu.VMEM((2, TOP_K, tt, d), F32), pltpu.SemaphoreType.DMA((2,))])
    return pl.pallas_call(
        functools.partial(_combine_kernel, gate_row=gate_row),
        grid_spec=grid_spec,
        out_shape=jax.ShapeDtypeStruct((n, d), F32),
        compiler_params=_cparams(1, 48),
        name="expert_combine",
    )(pos, y_sorted, top_w, x2, mod, tab)


def routing_tables(top_i, tm):
    n = top_i.shape[0]
    flat_e = top_i.reshape(-1)
    order = jnp.argsort(flat_e, stable=True).astype(jnp.int32)
    e_sorted = flat_e[order]
    counts = jnp.zeros((N_EXPERTS,), jnp.int32).at[flat_e].add(1)
    padded = ((counts + tm - 1) // tm) * tm
    ends_p = jnp.cumsum(padded)
    starts_p = ends_p - padded
    starts = jnp.cumsum(counts) - counts
    pos_sorted = starts_p[e_sorted] + (jnp.arange(n * TOP_K, dtype=jnp.int32) - starts[e_sorted])
    p_rows = n * TOP_K + N_EXPERTS * tm
    row_token = jnp.zeros((p_rows,), jnp.int32).at[pos_sorted].set(order // TOP_K)
    pos = jnp.zeros((n * TOP_K,), jnp.int32).at[order].set(pos_sorted)
    pos_choice_major = pos.reshape(n, TOP_K).T.reshape(-1)
    tile_start = jnp.arange(p_rows // tm, dtype=jnp.int32) * tm
    tile_expert = jnp.minimum(jnp.searchsorted(ends_p, tile_start, side="right"), N_EXPERTS - 1).astype(jnp.int32)
    n_used = (ends_p[-1] // tm).astype(jnp.int32).reshape(1)
    return tile_expert, row_token, n_used, pos_choice_major


def _alibi_slopes(n_heads):
    i = np.arange(1, n_heads + 1, dtype=np.float32)
    return np.exp2(-8.0 * i / n_heads).astype(np.float32)


def kernel(x, c, w_mod, b_mod, mod_table, norm_mix, w_in, conv_w, phi_k1, phi_k2, pe_k, phi_v1, phi_v2, pe_v,
           w_out, norm_ffn, w_router, b_router, w_gate, b_gate, w_up, b_up, w_down, b_down, norm_final):
    nb, seq, d = x.shape
    n = nb * seq
    depth = w_in.shape[0]
    cw = conv_w.shape[2]
    nw = d // 2
    g = NSA_KV_HEADS
    nsa_heads = nw // HEAD_DIM
    hg = nsa_heads // g
    dil_heads = (d - cw - nw) // HEAD_DIM
    n_dil = len(DIL_RATES)
    expert_tile = 256

    slopes = _alibi_slopes(n_dil * dil_heads + nsa_heads)
    slopes_dil = slopes[:n_dil * dil_heads].reshape(n_dil, dil_heads)
    slopes_nsa = slopes[n_dil * dil_heads:].reshape(g, hg)

    sizes = (cw,) * 3 + (nw,) + (g * HEAD_DIM,) * 6 + (nsa_heads * NSA_N_BRANCH,) \
        + (n_dil * dil_heads * HEAD_DIM,) + (n_dil * HEAD_DIM,) * 2
    off = np.concatenate([[0], np.cumsum(sizes)])
    conv_cols = slice(off[0], off[3])
    nsa_cols = slice(off[3], off[10])
    gate_cols = slice(off[10], off[11])
    dq_cols = slice(off[11], off[12])
    dkv_cols = slice(off[12], off[14])

    mod = mod_projection(c, w_mod, b_mod).reshape(nb, N_MOD, d)
    x2 = x.reshape(n, d)

    for l in range(depth):
        tab = mod_table[l]
        wl = w_in[l]
        w_gates = wl[:, gate_cols].reshape(d, g, hg, NSA_N_BRANCH).transpose(0, 1, 3, 2).reshape(d, g, NSA_N_BRANCH * hg)
        w_gates = jnp.pad(w_gates, ((0, 0), (0, 0), (0, LANES - NSA_N_BRANCH * hg))).reshape(d, g * LANES)

        h = modulated_norm(x2, norm_mix[l], mod, tab, seq, 0, 1, BF16)
        zc = matmul(h, wl[:, conv_cols].astype(BF16), F32)
        zn = matmul(h, wl[:, nsa_cols].astype(BF16), BF16)
        zg = matmul(h, w_gates.astype(BF16), F32, tn=g * LANES)
        zdq = matmul(h, wl[:, dq_cols].astype(BF16), BF16)
        zdkv = matmul(h, wl[:, dkv_cols].astype(BF16), BF16, tn=n_dil * HEAD_DIM)

        y_a = short_conv(zc, conv_w[l], seq)

        def chunks(which):
            cols = zn[:, nw + which * g * HEAD_DIM: nw + (which + 1) * g * HEAD_DIM]
            return (cols.reshape(nb, seq // NSA_CMP_STRIDE, NSA_CMP_STRIDE, g, HEAD_DIM)
                    .transpose(0, 3, 1, 2, 4).reshape(nb, g, seq // NSA_CMP_STRIDE, NSA_CMP_STRIDE * HEAD_DIM))
        kc = compress_tokens(chunks(0), phi_k1[l], phi_k2[l], pe_k[l])
        vc = compress_tokens(chunks(1), phi_v1[l], phi_v2[l], pe_v[l])
        y_b = nsa_attention(zn, kc, vc, zg, slopes_nsa, nb, seq, hg)

        parts = [dilated_group(zdq, zdkv, slopes_dil, gi, nb, seq, dil_heads) for gi in range(n_dil)]
        y_c = dilated_merge([p[0] for p in parts], [p[1] for p in parts], dil_heads)

        x2 = out_projection(y_a, y_b, y_c, w_out[l].astype(BF16), x2, mod, tab, seq, 2)

        h2, top_i, top_w = norm_router(x2, norm_ffn[l], mod, tab, w_router[l], b_router[l], seq, 3, 4)
        tile_expert, row_token, n_used, pos = routing_tables(top_i[:, :TOP_K], expert_tile)
        y_sorted = expert_ffn(h2, tile_expert, row_token, n_used,
                              w_gate[l].astype(BF16), b_gate[l], w_up[l].astype(BF16), b_up[l],
                              w_down[l].astype(BF16), b_down[l], expert_tile)
        x2 = expert_combine(y_sorted, pos, top_w, x2, mod, tab, seq, 5)

    return final_norm(x2, norm_final).reshape(nb, seq, d)
```
